```python
import math
import jax, jax.numpy as jnp
from jax import lax
import numpy as np

D_MODEL = 1024
BATCH = 8
SEQ = 8192
DEPTH = 2

N_MIXERS = 2
N_HEADS = 16
HEAD_DIM = D_MODEL // N_HEADS
CHUNK = 64
PAST_CHUNKS = 8
PAST = PAST_CHUNKS * CHUNK
BAND = (PAST_CHUNKS + 1) * CHUNK
REL_CLIP = 128
N_REL = 2 * REL_CLIP + 1
Q_BLOCK = 128
N_BAND_LAYERS = (DEPTH + 1) // 2
RMS_EPS = 1e-6
MASK_VALUE = -1e30

kernel_name = "hybrid_chunkband_stickbreaking_sandwich"


def rmsnorm(x, gain):
    xf = x.astype(jnp.float32)
    y = xf * lax.rsqrt(jnp.mean(xf * xf, axis=-1, keepdims=True) + RMS_EPS)
    return (y * gain.astype(jnp.float32)).astype(x.dtype)


def relative_bias(table):
    i = jnp.arange(CHUNK)[:, None]
    j = jnp.arange(BAND)[None, :]
    rel = i + PAST - j
    idx = jnp.clip(rel, -REL_CLIP, REL_CLIP) + REL_CLIP
    return table[:, idx]


def chunk_band_attention(q, k, v, bias):
    b, s, h, dh = q.shape
    n_chunks = s // CHUNK
    scale = 1.0 / math.sqrt(dh)
    pad = ((0, 0), (PAST, 0), (0, 0), (0, 0))
    kp = jnp.pad(k, pad)
    vp = jnp.pad(v, pad)
    bias_f = bias.astype(jnp.float32)

    def one_chunk(c):
        start = c * CHUNK
        q_c = lax.dynamic_slice_in_dim(q, start, CHUNK, axis=1)
        k_c = lax.dynamic_slice_in_dim(kp, start, BAND, axis=1)
        v_c = lax.dynamic_slice_in_dim(vp, start, BAND, axis=1)
        sc = jnp.einsum('bqhd,bkhd->bhqk', q_c, k_c).astype(jnp.float32) * scale + bias_f[None]
        valid = (start - PAST + jnp.arange(BAND)) >= 0
        sc = jnp.where(valid[None, None, None, :], sc, MASK_VALUE)
        p = jax.nn.softmax(sc, axis=-1)
        return jnp.einsum('bhqk,bkhd->bqhd', p.astype(v.dtype), v_c)

    o = lax.map(one_chunk, jnp.arange(n_chunks))
    return jnp.transpose(o, (1, 0, 2, 3, 4)).reshape(b, s, h * dh)


def stick_breaking_attention(q, k, v):
    b, s, h, dh = q.shape
    scale = 1.0 / math.sqrt(dh)
    outs = []
    for blk in range(s // Q_BLOCK):
        q0 = blk * Q_BLOCK
        kv_len = q0 + Q_BLOCK
        q_b = q[:, q0:q0 + Q_BLOCK]
        k_b = k[:, :kv_len]
        v_b = v[:, :kv_len]
        z = jnp.einsum('bqhd,bkhd->bhqk', q_b, k_b).astype(jnp.float32) * scale
        t_pos = q0 + jnp.arange(Q_BLOCK)
        causal = jnp.arange(kv_len)[None, :] < t_pos[:, None]
        log_beta = jax.nn.log_sigmoid(z)
        log_not = jnp.where(causal, log_beta - z, 0.0)
        log_surv = lax.cumsum(log_not, axis=3, reverse=True) - log_not
        a = jnp.where(causal, jnp.exp(log_beta + log_surv), 0.0)
        outs.append(jnp.einsum('bhqk,bkhd->bqhd', a.astype(v.dtype), v_b))
    return jnp.concatenate(outs, axis=1).reshape(b, s, h * dh)


def setup_inputs(seed: int = 0) -> dict:
    key = jax.random.key(seed)
    ks = jax.random.split(key, 6)
    x = jax.random.normal(ks[0], (BATCH, SEQ, D_MODEL), jnp.float32)
    norm_pre = 1.0 + 0.02 * jax.random.normal(ks[1], (DEPTH, D_MODEL), jnp.float32)
    w_in = jax.random.normal(ks[2], (DEPTH, D_MODEL, 4 * D_MODEL), jnp.float32) * D_MODEL ** -0.5
    rel_bias = 0.1 * jax.random.normal(ks[3], (N_BAND_LAYERS, N_HEADS, N_REL), jnp.float32)
    w_out = jax.random.normal(ks[4], (DEPTH, D_MODEL, D_MODEL), jnp.float32) * D_MODEL ** -0.5
    norm_post = 1.0 + 0.02 * jax.random.normal(ks[5], (DEPTH, D_MODEL), jnp.float32)
    return {"x": x, "norm_pre": norm_pre, "w_in": w_in, "rel_bias": rel_bias,
            "w_out": w_out, "norm_post": norm_post}


def reference(x, norm_pre, w_in, rel_bias, w_out, norm_post):
    b, s, d = x.shape
    for i in range(DEPTH):
        h = rmsnorm(x, norm_pre[i])
        u = jnp.einsum('bsd,de->bse', h, w_in[i])
        q, k, v, g = jnp.split(u, 4, axis=-1)
        q = q.reshape(b, s, N_HEADS, HEAD_DIM)
        k = k.reshape(b, s, N_HEADS, HEAD_DIM)
        v = v.reshape(b, s, N_HEADS, HEAD_DIM)
        if i % N_MIXERS == 0:
            o = chunk_band_attention(q, k, v, relative_bias(rel_bias[i // N_MIXERS]))
        else:
            o = stick_breaking_attention(q, k, v)
        y = jnp.einsum('bse,ed->bsd', o * jax.nn.silu(g), w_out[i])
        x = x + rmsnorm(y, norm_post[i])
    return x
```

```python
import functools
import math

import jax
import jax.numpy as jnp
from jax import lax
from jax.experimental import pallas as pl
from jax.experimental.pallas import tpu as pltpu

N_HEADS = 16
HEAD_DIM = 64
CHUNK = 64
PAST_CHUNKS = 8
PAST = PAST_CHUNKS * CHUNK
REL_CLIP = 128
RMS_EPS = 1e-6
MASK_VALUE = -1e30

HEAD_PAIR = 2 * HEAD_DIM
PROJ_ROWS = 512
ATT_Q = 256
ATT_K = 256
BAND_KEYS = PAST + ATT_Q
VMEM_LIMIT = 48 * 1024 * 1024

_NT = (((1,), (1,)), ((), ()))


def _rms_scale(x):
    return lax.rsqrt(jnp.mean(x * x, axis=-1, keepdims=True) + RMS_EPS)


def _in_proj_kernel(x_ref, gain_ref, w_ref, q_ref, k_ref, v_ref, g_ref):
    d = x_ref.shape[1]
    x = x_ref[...]
    h = (x * _rms_scale(x) * gain_ref[...]).astype(jnp.bfloat16)
    scale = 1.0 / math.sqrt(HEAD_DIM)
    u = jnp.dot(h, w_ref[:, 0:d], preferred_element_type=jnp.float32)
    q_ref[...] = (u * scale).astype(jnp.bfloat16)
    u = jnp.dot(h, w_ref[:, d:2 * d], preferred_element_type=jnp.float32)
    k_ref[...] = u.astype(jnp.bfloat16)
    u = jnp.dot(h, w_ref[:, 2 * d:3 * d], preferred_element_type=jnp.float32)
    v_ref[...] = u.astype(jnp.bfloat16)
    g_ref[...] = jnp.dot(h, w_ref[:, 3 * d:4 * d], preferred_element_type=jnp.float32)


def _in_proj(x2, gain, w):
    m, d = x2.shape
    row = pl.BlockSpec((PROJ_ROWS, d), lambda i: (i, 0))
    return pl.pallas_call(
        _in_proj_kernel,
        grid=(m // PROJ_ROWS,),
        in_specs=[row,
                  pl.BlockSpec((1, d), lambda i: (0, 0)),
                  pl.BlockSpec((d, 4 * d), lambda i: (0, 0))],
        out_specs=[row, row, row, row],
        out_shape=[jax.ShapeDtypeStruct((m, d), jnp.bfloat16)] * 3
        + [jax.ShapeDtypeStruct((m, d), jnp.float32)],
        compiler_params=pltpu.CompilerParams(
            dimension_semantics=("arbitrary",), vmem_limit_bytes=VMEM_LIMIT),
        name="in_proj",
    )(x2, gain, w)


def _out_proj_kernel(og_ref, w_ref, x_ref, gain_ref, o_ref):
    y = jnp.dot(og_ref[...], w_ref[...], preferred_element_type=jnp.float32)
    o_ref[...] = x_ref[...] + y * _rms_scale(y) * gain_ref[...]


def _out_proj(og2, w, x2, gain):
    m, d = x2.shape
    row = pl.BlockSpec((PROJ_ROWS, d), lambda i: (i, 0))
    return pl.pallas_call(
        _out_proj_kernel,
        grid=(m // PROJ_ROWS,),
        in_specs=[row,
                  pl.BlockSpec((d, d), lambda i: (0, 0)),
                  row,
                  pl.BlockSpec((1, d), lambda i: (0, 0))],
        out_specs=row,
        out_shape=jax.ShapeDtypeStruct((m, d), jnp.float32),
        compiler_params=pltpu.CompilerParams(
            dimension_semantics=("arbitrary",), vmem_limit_bytes=VMEM_LIMIT),
        name="out_proj",
    )(og2, w, x2, gain)


def _head_masks():
    lane = lax.broadcasted_iota(jnp.int32, (1, HEAD_PAIR), 1)
    return lane < HEAD_DIM, lane >= HEAD_DIM


def _silu(g):
    return g / (1.0 + jnp.exp(-g))


def _band_kernel(q_ref, k_ref, v_ref, g_ref, bm_ref, o_ref):
    qi = pl.program_id(2)
    ks = pl.multiple_of(jnp.maximum(qi * ATT_Q - PAST, 0), ATT_Q)
    q = q_ref[0]
    kt = k_ref[0, pl.ds(ks, BAND_KEYS), :]
    vt = v_ref[0, pl.ds(ks, BAND_KEYS), :]
    acc = jnp.zeros((ATT_Q, HEAD_PAIR), jnp.float32)
    for h, hm in enumerate(_head_masks()):
        km = jnp.where(hm, kt, jnp.zeros_like(kt))
        vm = jnp.where(hm, vt, jnp.zeros_like(vt))
        s = lax.dot_general(q, km, _NT, preferred_element_type=jnp.float32) + bm_ref[0, h]
        p = jnp.exp(s - jnp.max(s, axis=-1, keepdims=True))
        denom = jnp.sum(p, axis=-1, keepdims=True)
        o = jnp.dot(p.astype(jnp.bfloat16), vm, preferred_element_type=jnp.float32)
        acc = acc + o / denom
    o_ref[0] = (acc * _silu(g_ref[0])).astype(jnp.bfloat16)


def _band_bias_mask(table):
    i = jnp.arange(ATT_Q)[:, None]
    j = jnp.arange(BAND_KEYS)[None, :]
    out = []
    for off in (0, ATT_Q, 2 * ATT_Q):
        idx = jnp.clip(off + i - j, -REL_CLIP, REL_CLIP) + REL_CLIP
        qc = (off + i) // CHUNK
        kc = j // CHUNK
        valid = (kc <= qc) & (kc >= qc - PAST_CHUNKS)
        out.append(jnp.where(valid[None], table[:, idx], MASK_VALUE))
    return jnp.stack(out).astype(jnp.float32)


def _band_attention(q, k, v, g, bm):
    b, s, d = q.shape
    blk = pl.BlockSpec((1, ATT_Q, HEAD_PAIR), lambda hp, bi, qi: (bi, qi, hp))
    seq = pl.BlockSpec((1, s, HEAD_PAIR), lambda hp, bi, qi: (bi, 0, hp))
    return pl.pallas_call(
        _band_kernel,
        grid=(d // HEAD_PAIR, b, s // ATT_Q),
        in_specs=[blk, seq, seq, blk,
                  pl.BlockSpec((1, 2, ATT_Q, BAND_KEYS),
                               lambda hp, bi, qi: (jnp.minimum(qi, 2), hp, 0, 0))],
        out_specs=blk,
        out_shape=jax.ShapeDtypeStruct((b, s, d), jnp.bfloat16),
        compiler_params=pltpu.CompilerParams(
            dimension_semantics=("arbitrary",) * 3, vmem_limit_bytes=VMEM_LIMIT),
        name="band_attention",
    )(q, k, v, g, bm)


def _softplus(z):
    return jnp.maximum(z, 0.0) + jnp.log(1.0 + jnp.exp(-jnp.abs(z)))


def _sb_kernel(q_ref, k_ref, v_ref, g_ref, o_ref, acc_ref, carry_ref):
    qi = pl.program_id(2)
    q = q_ref[0]
    masks = _head_masks()
    kk = lax.broadcasted_iota(jnp.int32, (ATT_K, ATT_K), 0)
    nn = lax.broadcasted_iota(jnp.int32, (ATT_K, ATT_K), 1)
    strict_after = (kk > nn).astype(jnp.bfloat16)
    causal = nn < kk

    def tile(j, diagonal):
        start = pl.multiple_of(j * ATT_K, ATT_K)
        kt = k_ref[0, pl.ds(start, ATT_K), :]
        vt = v_ref[0, pl.ds(start, ATT_K), :]
        for h, hm in enumerate(masks):
            km = jnp.where(hm, kt, jnp.zeros_like(kt))
            vm = jnp.where(hm, vt, jnp.zeros_like(vt))
            z = lax.dot_general(q, km, _NT, preferred_element_type=jnp.float32)
            sp = _softplus(z)
            if diagonal:
                sp = jnp.where(causal, sp, 0.0)
            log_beta = z - sp
            carry = carry_ref[h]
            after = jnp.dot(sp.astype(jnp.bfloat16), strict_after,
                            preferred_element_type=jnp.float32) + carry
            a = jnp.exp(log_beta - after)
            if diagonal:
                a = jnp.where(causal, a, 0.0)
            acc_ref[...] += jnp.dot(a.astype(jnp.bfloat16), vm,
                                    preferred_element_type=jnp.float32)
            carry_ref[h] = after[:, 0:1] + sp[:, 0:1]

    acc_ref[...] = jnp.zeros_like(acc_ref)
    carry_ref[...] = jnp.zeros_like(carry_ref)
    tile(qi, True)

    def body(jj, c):
        tile(qi - 1 - jj, False)
        return c

    lax.fori_loop(0, qi, body, 0)
    o_ref[0] = (acc_ref[...] * _silu(g_ref[0])).astype(jnp.bfloat16)


def _stick_breaking_attention(q, k, v, g):
    b, s, d = q.shape
    blk = pl.BlockSpec((1, ATT_Q, HEAD_PAIR), lambda bi, hp, qi: (bi, qi, hp))
    seq = pl.BlockSpec((1, s, HEAD_PAIR), lambda bi, hp, qi: (bi, 0, hp))
    return pl.pallas_call(
        _sb_kernel,
        grid=(b, d // HEAD_PAIR, s // ATT_Q),
        in_specs=[blk, seq, seq, blk],
        out_specs=blk,
        out_shape=jax.ShapeDtypeStruct((b, s, d), jnp.bfloat16),
        scratch_shapes=[pltpu.VMEM((ATT_Q, HEAD_PAIR), jnp.float32),
                        pltpu.VMEM((2, ATT_Q, 1), jnp.float32)],
        compiler_params=pltpu.CompilerParams(
            dimension_semantics=("arbitrary",) * 3, vmem_limit_bytes=VMEM_LIMIT),
        name="stick_breaking_attention",
    )(q, k, v, g)


def kernel(x, norm_pre, w_in, rel_bias, w_out, norm_post):
    b, s, d = x.shape
    depth = w_in.shape[0]
    x2 = x.reshape(b * s, d)
    for i in range(depth):
        q, k, v, g = _in_proj(x2, norm_pre[i][None, :], w_in[i].astype(jnp.bfloat16))
        q, k, v, g = (t.reshape(b, s, d) for t in (q, k, v, g))
        if i % 2 == 0:
            og = _band_attention(q, k, v, g, _band_bias_mask(rel_bias[i // 2]))
        else:
            og = _stick_breaking_attention(q, k, v, g)
        x2 = _out_proj(og.reshape(b * s, d), w_out[i].astype(jnp.bfloat16), x2,
                       norm_post[i][None, :])
    return x2.reshape(b, s, d)
```

```python
import functools
import math

import jax
import jax.numpy as jnp
from jax import lax
from jax.experimental import pallas as pl
from jax.experimental.pallas import tpu as pltpu

N_HEADS = 16
HEAD_DIM = 64
CHUNK = 64
PAST_CHUNKS = 8
PAST = PAST_CHUNKS * CHUNK
REL_CLIP = 128
RMS_EPS = 1e-6
MASK_VALUE = -1e30
LOG2E = math.log2(math.e)

HEAD_PAIR = 2 * HEAD_DIM
PROJ_ROWS = 512
ATT = 256
BAND_KEYS = PAST + ATT
BAND_TILES = BAND_KEYS // ATT
VMEM_LIMIT = 48 * 1024 * 1024

_NT = (((1,), (1,)), ((), ()))


def _rms_scale(x):
    return lax.rsqrt(jnp.mean(x * x, axis=-1, keepdims=True) + RMS_EPS)


def _in_proj_kernel(q_scale, x_ref, gain_ref, wqt_ref, wk_ref, wvt_ref, wg_ref,
                    qt_ref, k_ref, vt_ref, g_ref):
    x = x_ref[...]
    h = (x * _rms_scale(x) * gain_ref[...]).astype(jnp.bfloat16)
    k_ref[...] = jnp.dot(h, wk_ref[...], preferred_element_type=jnp.float32).astype(jnp.bfloat16)
    g_ref[...] = jnp.dot(h, wg_ref[...], preferred_element_type=jnp.float32)
    for t in range(PROJ_ROWS // ATT):
        ht = h[t * ATT:(t + 1) * ATT]
        qt = lax.dot_general(wqt_ref[...], ht, _NT, preferred_element_type=jnp.float32)
        qt_ref[t] = (qt * q_scale).astype(jnp.bfloat16)
        vt = lax.dot_general(wvt_ref[...], ht, _NT, preferred_element_type=jnp.float32)
        vt_ref[t] = vt.astype(jnp.bfloat16)


def _in_proj(x2, gain, w, q_scale):
    m, d = x2.shape
    wb = w.astype(jnp.bfloat16)
    wqt, wk, wvt, wg = wb[:, 0:d].T, wb[:, d:2 * d], wb[:, 2 * d:3 * d].T, wb[:, 3 * d:4 * d]
    row = pl.BlockSpec((PROJ_ROWS, d), lambda i: (i, 0))
    wspec = pl.BlockSpec((d, d), lambda i: (0, 0))
    tspec = pl.BlockSpec((PROJ_ROWS // ATT, d, ATT), lambda i: (i, 0, 0))
    tshape = jax.ShapeDtypeStruct((m // ATT, d, ATT), jnp.bfloat16)
    return pl.pallas_call(
        functools.partial(_in_proj_kernel, q_scale),
        grid=(m // PROJ_ROWS,),
        in_specs=[row, pl.BlockSpec((1, d), lambda i: (0, 0)), wspec, wspec, wspec, wspec],
        out_specs=[tspec, row, tspec, row],
        out_shape=[tshape, jax.ShapeDtypeStruct((m, d), jnp.bfloat16), tshape,
                   jax.ShapeDtypeStruct((m, d), jnp.float32)],
        compiler_params=pltpu.CompilerParams(
            dimension_semantics=("arbitrary",), vmem_limit_bytes=VMEM_LIMIT),
        name="in_proj",
    )(x2, gain, wqt, wk, wvt, wg)


def _out_proj_kernel(og_ref, w_ref, x_ref, gain_ref, o_ref):
    y = jnp.dot(og_ref[...], w_ref[...], preferred_element_type=jnp.float32)
    o_ref[...] = x_ref[...] + y * _rms_scale(y) * gain_ref[...]


def _out_proj(og2, w, x2, gain):
    m, d = x2.shape
    row = pl.BlockSpec((PROJ_ROWS, d), lambda i: (i, 0))
    return pl.pallas_call(
        _out_proj_kernel,
        grid=(m // PROJ_ROWS,),
        in_specs=[row,
                  pl.BlockSpec((d, d), lambda i: (0, 0)),
                  row,
                  pl.BlockSpec((1, d), lambda i: (0, 0))],
        out_specs=row,
        out_shape=jax.ShapeDtypeStruct((m, d), jnp.float32),
        compiler_params=pltpu.CompilerParams(
            dimension_semantics=("arbitrary",), vmem_limit_bytes=VMEM_LIMIT),
        name="out_proj",
    )(og2, w, x2, gain)


def _split_heads_t(qt):
    row = lax.broadcasted_iota(jnp.int32, (HEAD_PAIR, 1), 0)
    zero = jnp.zeros_like(qt)
    return jnp.where(row < HEAD_DIM, qt, zero), jnp.where(row >= HEAD_DIM, qt, zero)


def _gate_and_store(ot_ref, g_ref, o_ref):
    g = g_ref[0]
    o_ref[0] = (ot_ref[...].T * (g / (1.0 + jnp.exp(-g)))).astype(jnp.bfloat16)


def _band_kernel(qt_ref, k_ref, vt_ref, g_ref, bm_ref, o_ref, ot_ref):
    qi = pl.program_id(2)
    t0 = jnp.maximum(qi - PAST // ATT, 0)
    kt = k_ref[0, pl.ds(pl.multiple_of(t0 * ATT, ATT), BAND_KEYS), :]
    for h, qth in enumerate(_split_heads_t(qt_ref[0])):
        s = jnp.dot(kt, qth, preferred_element_type=jnp.float32) + bm_ref[h]
        p = jnp.exp(s - jnp.max(s, axis=0, keepdims=True))
        denom = jnp.sum(p, axis=0, keepdims=True)
        p = p.astype(jnp.bfloat16)
        o = jnp.zeros((HEAD_DIM, ATT), jnp.float32)
        for t in range(BAND_TILES):
            o = o + jnp.dot(vt_ref[0, t0 + t, h * HEAD_DIM:(h + 1) * HEAD_DIM, :],
                            p[t * ATT:(t + 1) * ATT], preferred_element_type=jnp.float32)
        ot_ref[h * HEAD_DIM:(h + 1) * HEAD_DIM, :] = o / denom
    _gate_and_store(ot_ref, g_ref, o_ref)


def _toeplitz(g, rows, cols):
    n = g.shape[-1]
    flat = jnp.broadcast_to(g[..., None, :], g.shape[:-1] + (rows, n)).reshape(g.shape[:-1] + (rows * n,))
    return flat[..., :rows * (n - 1)].reshape(g.shape[:-1] + (rows, n - 1))[..., :cols]


def _band_bias_mask_t(table):
    n = 2 * BAND_KEYS
    m = jnp.arange(n)
    rel = jnp.where(m < BAND_KEYS, m, m - n)
    g = table[:, jnp.clip(rel, -REL_CLIP, REL_CLIP) + REL_CLIP]
    bias = _toeplitz(g, BAND_KEYS, BAND_KEYS)
    kc = jnp.arange(BAND_KEYS)[:, None] // CHUNK
    qc = jnp.arange(BAND_KEYS)[None, :] // CHUNK
    valid = (kc <= qc) & (kc >= qc - PAST_CHUNKS)
    return jnp.where(valid[None], bias, MASK_VALUE).astype(jnp.float32)


def _band_attention(qt, k, vt, g, bm):
    b, s, d = k.shape
    nq = s // ATT
    return pl.pallas_call(
        _band_kernel,
        grid=(d // HEAD_PAIR, b, nq),
        in_specs=[pl.BlockSpec((1, HEAD_PAIR, ATT), lambda hp, bi, qi: (bi * nq + qi, hp, 0)),
                  pl.BlockSpec((1, s, HEAD_PAIR), lambda hp, bi, qi: (bi, 0, hp)),
                  pl.BlockSpec((1, nq, HEAD_PAIR, ATT), lambda hp, bi, qi: (bi, 0, hp, 0)),
                  pl.BlockSpec((1, ATT, HEAD_PAIR), lambda hp, bi, qi: (bi, qi, hp)),
                  pl.BlockSpec((2, BAND_KEYS, ATT),
                               lambda hp, bi, qi: (hp, 0, jnp.minimum(qi, PAST // ATT)))],
        out_specs=pl.BlockSpec((1, ATT, HEAD_PAIR), lambda hp, bi, qi: (bi, qi, hp)),
        out_shape=jax.ShapeDtypeStruct((b, s, d), jnp.bfloat16),
        scratch_shapes=[pltpu.VMEM((HEAD_PAIR, ATT), jnp.float32)],
        compiler_params=pltpu.CompilerParams(
            dimension_semantics=("arbitrary",) * 3, vmem_limit_bytes=VMEM_LIMIT),
        name="band_attention",
    )(qt, k, vt.reshape(b, nq, d, ATT), g, bm)


def _softplus2(z):
    neg_abs = pltpu.bitcast(pltpu.bitcast(z, jnp.uint32) | jnp.uint32(0x80000000), jnp.float32)
    return jnp.maximum(z, 0.0) + jnp.log(1.0 + jnp.exp2(neg_abs)) * LOG2E


def _sb_kernel(qt_ref, k_ref, vt_ref, g_ref, o_ref,
               ot_ref, carry_ref, colsum_ref, sp_ref, lb_ref, d_ref):
    qi = pl.program_id(2)
    qts = _split_heads_t(qt_ref[0])
    ss = lax.broadcasted_iota(jnp.int32, (ATT, ATT), 0)
    cc = lax.broadcasted_iota(jnp.int32, (ATT, ATT), 1)
    strict_after = (cc > ss).astype(jnp.bfloat16)
    causal = ss < cc

    def scores_mm(j):
        kt = k_ref[0, pl.ds(pl.multiple_of(j * ATT, ATT), ATT), :]
        return [jnp.dot(kt, qth, preferred_element_type=jnp.float32) for qth in qts]

    def scores_tail(zs, diagonal):
        for h, z in enumerate(zs):
            sp = _softplus2(z)
            lb = z - sp
            if diagonal:
                sp = jnp.where(causal, sp, 0.0)
                lb = jnp.where(causal, lb, MASK_VALUE)
            sp_ref[h] = sp.astype(jnp.bfloat16)
            lb_ref[h] = lb

    def suffix_mm():
        return [jnp.dot(strict_after, sp_ref[h], preferred_element_type=jnp.float32)
                for h in range(2)]

    def suffix_tail(afters):
        for h, after in enumerate(afters):
            d_ref[h] = lb_ref[h] - after
            colsum_ref[h] = after[0:1, :] + sp_ref[h, 0:1, :].astype(jnp.float32)

    def values(j):
        for h in range(2):
            carry = carry_ref[h]
            a = jnp.exp2(d_ref[h]).astype(jnp.bfloat16)
            rows = slice(h * HEAD_DIM, (h + 1) * HEAD_DIM)
            pv = jnp.dot(vt_ref[0, j, rows, :], a, preferred_element_type=jnp.float32)
            ot_ref[rows, :] += pv * jnp.exp2(-carry)
            carry_ref[h] = carry + colsum_ref[h]

    ot_ref[...] = jnp.zeros_like(ot_ref)
    carry_ref[...] = jnp.zeros_like(carry_ref)
    scores_tail(scores_mm(qi), True)
    zs = scores_mm(jnp.maximum(qi - 1, 0))
    suffix_tail(suffix_mm())
    scores_tail(zs, False)

    def body(t, c):
        zs = scores_mm(jnp.maximum(qi - 2 - t, 0))
        afters = suffix_mm()
        values(qi - t)
        suffix_tail(afters)
        scores_tail(zs, False)
        return c

    lax.fori_loop(0, qi + 1, body, 0)
    _gate_and_store(ot_ref, g_ref, o_ref)


def _stick_breaking_attention(qt, k, vt, g):
    b, s, d = k.shape
    nq = s // ATT
    return pl.pallas_call(
        _sb_kernel,
        grid=(b, d // HEAD_PAIR, nq),
        in_specs=[pl.BlockSpec((1, HEAD_PAIR, ATT), lambda bi, hp, qi: (bi * nq + qi, hp, 0)),
                  pl.BlockSpec((1, s, HEAD_PAIR), lambda bi, hp, qi: (bi, 0, hp)),
                  pl.BlockSpec((1, nq, HEAD_PAIR, ATT), lambda bi, hp, qi: (bi, 0, hp, 0)),
                  pl.BlockSpec((1, ATT, HEAD_PAIR), lambda bi, hp, qi: (bi, qi, hp))],
        out_specs=pl.BlockSpec((1, ATT, HEAD_PAIR), lambda bi, hp, qi: (bi, qi, hp)),
        out_shape=jax.ShapeDtypeStruct((b, s, d), jnp.bfloat16),
        scratch_shapes=[pltpu.VMEM((HEAD_PAIR, ATT), jnp.float32),
                        pltpu.VMEM((2, 1, ATT), jnp.float32),
                        pltpu.VMEM((2, 1, ATT), jnp.float32),
                        pltpu.VMEM((2, ATT, ATT), jnp.bfloat16),
                        pltpu.VMEM((2, ATT, ATT), jnp.float32),
                        pltpu.VMEM((2, ATT, ATT), jnp.float32)],
        compiler_params=pltpu.CompilerParams(
            dimension_semantics=("arbitrary",) * 3, vmem_limit_bytes=VMEM_LIMIT),
        name="stick_breaking_attention",
    )(qt, k, vt.reshape(b, nq, d, ATT), g)


def kernel(x, norm_pre, w_in, rel_bias, w_out, norm_post):
    b, s, d = x.shape
    depth = w_in.shape[0]
    x2 = x.reshape(b * s, d)
    scale = 1.0 / math.sqrt(HEAD_DIM)
    for i in range(depth):
        band = i % 2 == 0
        qt, k, vt, g = _in_proj(x2, norm_pre[i][None, :], w_in[i],
                                scale if band else scale * LOG2E)
        k, g = k.reshape(b, s, d), g.reshape(b, s, d)
        if band:
            og = _band_attention(qt, k, vt, g, _band_bias_mask_t(rel_bias[i // 2]))
        else:
            og = _stick_breaking_attention(qt, k, vt, g)
        x2 = _out_proj(og.reshape(b * s, d), w_out[i].astype(jnp.bfloat16), x2,
                       norm_post[i][None, :])
    return x2.reshape(b, s, d)
```

```python
import functools
import math

import jax
import jax.numpy as jnp
from jax import lax
from jax.experimental import pallas as pl
from jax.experimental.pallas import tpu as pltpu

N_HEADS = 16
HEAD_DIM = 64
CHUNK = 64
PAST_CHUNKS = 8
PAST = PAST_CHUNKS * CHUNK
REL_CLIP = 128
RMS_EPS = 1e-6
MASK_VALUE = -1e30
LOG2E = math.log2(math.e)

HEAD_PAIR = 2 * HEAD_DIM
PROJ_ROWS = 512
ATT = 256
BAND_TILES = PAST // ATT + 1
VMEM_LIMIT = 48 * 1024 * 1024

_NT = (((1,), (1,)), ((), ()))


def _rms_scale(x):
    return lax.rsqrt(jnp.mean(x * x, axis=-1, keepdims=True) + RMS_EPS)


def _in_proj_kernel(q_scale, x_ref, gain_ref, wqt_ref, wk_ref, wvt_ref, wg_ref,
                    qt_ref, k_ref, vt_ref, g_ref):
    x = x_ref[...]
    h = (x * _rms_scale(x) * gain_ref[...]).astype(jnp.bfloat16)
    k_ref[...] = jnp.dot(h, wk_ref[...], preferred_element_type=jnp.float32).astype(jnp.bfloat16)
    g_ref[...] = jnp.dot(h, wg_ref[...], preferred_element_type=jnp.float32)
    for t in range(PROJ_ROWS // ATT):
        ht = h[t * ATT:(t + 1) * ATT]
        qt = lax.dot_general(wqt_ref[...], ht, _NT, preferred_element_type=jnp.float32)
        qt_ref[t] = (qt * q_scale).astype(jnp.bfloat16)
        vt = lax.dot_general(wvt_ref[...], ht, _NT, preferred_element_type=jnp.float32)
        vt_ref[t] = vt.astype(jnp.bfloat16)


def _in_proj(x2, gain, w, q_scale):
    m, d = x2.shape
    wb = w.astype(jnp.bfloat16)
    wqt, wk, wvt, wg = wb[:, 0:d].T, wb[:, d:2 * d], wb[:, 2 * d:3 * d].T, wb[:, 3 * d:4 * d]
    row = pl.BlockSpec((PROJ_ROWS, d), lambda i: (i, 0))
    wspec = pl.BlockSpec((d, d), lambda i: (0, 0))
    tspec = pl.BlockSpec((PROJ_ROWS // ATT, d, ATT), lambda i: (i, 0, 0))
    tshape = jax.ShapeDtypeStruct((m // ATT, d, ATT), jnp.bfloat16)
    return pl.pallas_call(
        functools.partial(_in_proj_kernel, q_scale),
        grid=(m // PROJ_ROWS,),
        in_specs=[row, pl.BlockSpec((1, d), lambda i: (0, 0)), wspec, wspec, wspec, wspec],
        out_specs=[tspec, row, tspec, row],
        out_shape=[tshape, jax.ShapeDtypeStruct((m, d), jnp.bfloat16), tshape,
                   jax.ShapeDtypeStruct((m, d), jnp.float32)],
        compiler_params=pltpu.CompilerParams(
            dimension_semantics=("arbitrary",), vmem_limit_bytes=VMEM_LIMIT),
        name="in_proj",
    )(x2, gain, wqt, wk, wvt, wg)


def _out_proj_kernel(og_ref, w_ref, x_ref, gain_ref, o_ref):
    y = jnp.dot(og_ref[...], w_ref[...], preferred_element_type=jnp.float32)
    o_ref[...] = x_ref[...] + y * _rms_scale(y) * gain_ref[...]


def _out_proj(og2, w, x2, gain):
    m, d = x2.shape
    row = pl.BlockSpec((PROJ_ROWS, d), lambda i: (i, 0))
    return pl.pallas_call(
        _out_proj_kernel,
        grid=(m // PROJ_ROWS,),
        in_specs=[row,
                  pl.BlockSpec((d, d), lambda i: (0, 0)),
                  row,
                  pl.BlockSpec((1, d), lambda i: (0, 0))],
        out_specs=row,
        out_shape=jax.ShapeDtypeStruct((m, d), jnp.float32),
        compiler_params=pltpu.CompilerParams(
            dimension_semantics=("arbitrary",), vmem_limit_bytes=VMEM_LIMIT),
        name="out_proj",
    )(og2, w, x2, gain)


def _split_heads_t(qt):
    row = lax.broadcasted_iota(jnp.int32, (HEAD_PAIR, 1), 0)
    zero = jnp.zeros_like(qt)
    return jnp.where(row < HEAD_DIM, qt, zero), jnp.where(row >= HEAD_DIM, qt, zero)


def _gated(ot, g):
    return (ot.T * (g / (1.0 + jnp.exp(-g)))).astype(jnp.bfloat16)


def _band_bias_kernel(g_ref, o_ref):
    kc = lax.broadcasted_iota(jnp.int32, (ATT, ATT), 0) // CHUNK
    qc = lax.broadcasted_iota(jnp.int32, (ATT, ATT), 1) // CHUNK
    for dlt in range(BAND_TILES):
        g = jnp.broadcast_to(g_ref[0, dlt], (ATT, 2 * ATT))
        toeplitz = pltpu.roll(g, 0, 1, stride=1, stride_axis=0)[:, :ATT]
        back = qc + dlt * (ATT // CHUNK) - kc
        valid = (back >= 0) & (back <= PAST_CHUNKS)
        o_ref[0, dlt] = jnp.where(valid, toeplitz, MASK_VALUE)
    o_ref[0, BAND_TILES] = jnp.full((ATT, ATT), MASK_VALUE, jnp.float32)


def _band_bias_tiles(table):
    nh = table.shape[0]
    m = jnp.arange(2 * ATT)
    rel = jnp.where(m < ATT, m, m - 2 * ATT)[None, :] + ATT * jnp.arange(BAND_TILES)[:, None]
    g = table[:, jnp.clip(rel, -REL_CLIP, REL_CLIP) + REL_CLIP]
    return pl.pallas_call(
        _band_bias_kernel,
        grid=(nh,),
        in_specs=[pl.BlockSpec((1, BAND_TILES, 1, 2 * ATT), lambda h: (h, 0, 0, 0))],
        out_specs=pl.BlockSpec((1, BAND_TILES + 1, ATT, ATT), lambda h: (h, 0, 0, 0)),
        out_shape=jax.ShapeDtypeStruct((nh, BAND_TILES + 1, ATT, ATT), jnp.float32),
        compiler_params=pltpu.CompilerParams(dimension_semantics=("arbitrary",)),
        name="band_bias_tiles",
    )(g[:, :, None, :].astype(jnp.float32))


def _band_kernel(qt_ref, k_ref, vt_ref, g_ref, bm_ref, o_ref):
    qi = pl.program_id(2)
    t0 = jnp.maximum(qi - (BAND_TILES - 1), 0)
    kt = k_ref[0, pl.ds(pl.multiple_of(t0 * ATT, ATT), BAND_TILES * ATT), :]
    scores = [jnp.dot(kt, qth, preferred_element_type=jnp.float32)
              for qth in _split_heads_t(qt_ref[0])]
    outs = []
    for h, s in enumerate(scores):
        tiles = []
        for t in range(BAND_TILES):
            dlt = qi - (t0 + t)
            tiles.append(s[t * ATT:(t + 1) * ATT]
                         + bm_ref[h, jnp.where(dlt < 0, BAND_TILES, dlt)])
        top = functools.reduce(jnp.maximum, [jnp.max(x, axis=0, keepdims=True) for x in tiles])
        ps = [jnp.exp(x - top) for x in tiles]
        denom = sum(jnp.sum(p, axis=0, keepdims=True) for p in ps)
        rows = slice(h * HEAD_DIM, (h + 1) * HEAD_DIM)
        o = sum(jnp.dot(vt_ref[0, t0 + t, rows, :], p.astype(jnp.bfloat16),
                        preferred_element_type=jnp.float32) for t, p in enumerate(ps))
        outs.append(o / denom)
    o_ref[0] = _gated(jnp.concatenate(outs, axis=0), g_ref[0])


def _band_attention(qt, k, vt, g, bm):
    b, s, d = k.shape
    nq = s // ATT
    return pl.pallas_call(
        _band_kernel,
        grid=(d // HEAD_PAIR, b, nq),
        in_specs=[pl.BlockSpec((1, HEAD_PAIR, ATT), lambda hp, bi, qi: (bi * nq + qi, hp, 0)),
                  pl.BlockSpec((1, s, HEAD_PAIR), lambda hp, bi, qi: (bi, 0, hp)),
                  pl.BlockSpec((1, nq, HEAD_PAIR, ATT), lambda hp, bi, qi: (bi, 0, hp, 0)),
                  pl.BlockSpec((1, ATT, HEAD_PAIR), lambda hp, bi, qi: (bi, qi, hp)),
                  pl.BlockSpec((2, BAND_TILES + 1, ATT, ATT), lambda hp, bi, qi: (hp, 0, 0, 0))],
        out_specs=pl.BlockSpec((1, ATT, HEAD_PAIR), lambda hp, bi, qi: (bi, qi, hp)),
        out_shape=jax.ShapeDtypeStruct((b, s, d), jnp.bfloat16),
        compiler_params=pltpu.CompilerParams(
            dimension_semantics=("arbitrary",) * 3, vmem_limit_bytes=VMEM_LIMIT),
        name="band_attention",
    )(qt, k, vt.reshape(b, nq, d, ATT), g, bm)


def _softplus2(z):
    neg_abs = pltpu.bitcast(pltpu.bitcast(z, jnp.uint32) | jnp.uint32(0x80000000), jnp.float32)
    return jnp.maximum(z, 0.0) + jnp.log(1.0 + jnp.exp2(neg_abs)) * LOG2E


def _sb_kernel(qt_ref, k_ref, vt_ref, g_ref, o_ref,
               ot_ref, carry_ref, colsum_ref, sp_ref, lb_ref, d_ref):
    qi = pl.program_id(2)
    qts = _split_heads_t(qt_ref[0])
    ss = lax.broadcasted_iota(jnp.int32, (ATT, ATT), 0)
    cc = lax.broadcasted_iota(jnp.int32, (ATT, ATT), 1)
    strict_after = (cc > ss).astype(jnp.bfloat16)
    causal = ss < cc

    def scores_mm(j):
        kt = k_ref[0, pl.ds(pl.multiple_of(j * ATT, ATT), ATT), :]
        return [jnp.dot(kt, qth, preferred_element_type=jnp.float32) for qth in qts]

    def scores_tail(zs, diagonal):
        for h, z in enumerate(zs):
            sp = _softplus2(z)
            lb = z - sp
            if diagonal:
                sp = jnp.where(causal, sp, 0.0)
                lb = jnp.where(causal, lb, MASK_VALUE)
            sp_ref[h] = sp.astype(jnp.bfloat16)
            lb_ref[h] = lb

    def suffix_mm():
        return [jnp.dot(strict_after, sp_ref[h], preferred_element_type=jnp.float32)
                for h in range(2)]

    def suffix_tail(afters):
        for h, after in enumerate(afters):
            d_ref[h] = lb_ref[h] - after
            colsum_ref[h] = after[0:1, :] + sp_ref[h, 0:1, :].astype(jnp.float32)

    def values(j):
        for h in range(2):
            carry = carry_ref[h]
            a = jnp.exp2(d_ref[h]).astype(jnp.bfloat16)
            rows = slice(h * HEAD_DIM, (h + 1) * HEAD_DIM)
            pv = jnp.dot(vt_ref[0, j, rows, :], a, preferred_element_type=jnp.float32)
            ot_ref[rows, :] += pv * jnp.exp2(-carry)
            carry_ref[h] = carry + colsum_ref[h]

    ot_ref[...] = jnp.zeros_like(ot_ref)
    carry_ref[...] = jnp.zeros_like(carry_ref)
    scores_tail(scores_mm(qi), True)
    zs = scores_mm(jnp.maximum(qi - 1, 0))
    suffix_tail(suffix_mm())
    scores_tail(zs, False)

    def body(t, c):
        zs = scores_mm(jnp.maximum(qi - 2 - t, 0))
        afters = suffix_mm()
        values(qi - t)
        suffix_tail(afters)
        scores_tail(zs, False)
        return c

    lax.fori_loop(0, qi + 1, body, 0)
    o_ref[0] = _gated(ot_ref[...], g_ref[0])


def _stick_breaking_attention(qt, k, vt, g):
    b, s, d = k.shape
    nq = s // ATT
    return pl.pallas_call(
        _sb_kernel,
        grid=(b, d // HEAD_PAIR, nq),
        in_specs=[pl.BlockSpec((1, HEAD_PAIR, ATT), lambda bi, hp, qi: (bi * nq + qi, hp, 0)),
                  pl.BlockSpec((1, s, HEAD_PAIR), lambda bi, hp, qi: (bi, 0, hp)),
                  pl.BlockSpec((1, nq, HEAD_PAIR, ATT), lambda bi, hp, qi: (bi, 0, hp, 0)),
                  pl.BlockSpec((1, ATT, HEAD_PAIR), lambda bi, hp, qi: (bi, qi, hp))],
        out_specs=pl.BlockSpec((1, ATT, HEAD_PAIR), lambda bi, hp, qi: (bi, qi, hp)),
        out_shape=jax.ShapeDtypeStruct((b, s, d), jnp.bfloat16),
        scratch_shapes=[pltpu.VMEM((HEAD_PAIR, ATT), jnp.float32),
                        pltpu.VMEM((2, 1, ATT), jnp.float32),
                        pltpu.VMEM((2, 1, ATT), jnp.float32),
                        pltpu.VMEM((2, ATT, ATT), jnp.bfloat16),
                        pltpu.VMEM((2, ATT, ATT), jnp.float32),
                        pltpu.VMEM((2, ATT, ATT), jnp.float32)],
        compiler_params=pltpu.CompilerParams(
            dimension_semantics=("arbitrary",) * 3, vmem_limit_bytes=VMEM_LIMIT),
        name="stick_breaking_attention",
    )(qt, k, vt.reshape(b, nq, d, ATT), g)


def kernel(x, norm_pre, w_in, rel_bias, w_out, norm_post):
    b, s, d = x.shape
    depth = w_in.shape[0]
    x2 = x.reshape(b * s, d)
    scale = 1.0 / math.sqrt(HEAD_DIM)
    for i in range(depth):
        band = i % 2 == 0
        qt, k, vt, g = _in_proj(x2, norm_pre[i][None, :], w_in[i],
                                scale if band else scale * LOG2E)
        k, g = k.reshape(b, s, d), g.reshape(b, s, d)
        if band:
            og = _band_attention(qt, k, vt, g, _band_bias_tiles(rel_bias[i // 2]))
        else:
            og = _stick_breaking_attention(qt, k, vt, g)
        x2 = _out_proj(og.reshape(b * s, d), w_out[i].astype(jnp.bfloat16), x2,
                       norm_post[i][None, :])
    return x2.reshape(b, s, d)
```

```python
import functools
import math

import jax
import jax.numpy as jnp
from jax import lax
from jax.experimental import pallas as pl
from jax.experimental.pallas import tpu as pltpu

N_HEADS = 16
HEAD_DIM = 64
CHUNK = 64
PAST_CHUNKS = 8
PAST = PAST_CHUNKS * CHUNK
REL_CLIP = 128
RMS_EPS = 1e-6
MASK_VALUE = -1e30
LOG2E = math.log2(math.e)

HEAD_PAIR = 2 * HEAD_DIM
PROJ_ROWS = 512
ATT = 256
BAND_TILES = PAST // ATT + 1
VMEM_LIMIT = 48 * 1024 * 1024

_NT = (((1,), (1,)), ((), ()))


def _rms_scale(x):
    return lax.rsqrt(jnp.mean(x * x, axis=-1, keepdims=True) + RMS_EPS)


def _in_proj_kernel(q_scale, x_ref, gain_ref, wqt_ref, wk_ref, wvt_ref, wg_ref,
                    qt_ref, k_ref, vt_ref, g_ref):
    x = x_ref[...]
    h = (x * _rms_scale(x) * gain_ref[...]).astype(jnp.bfloat16)
    k_ref[...] = jnp.dot(h, wk_ref[...], preferred_element_type=jnp.float32).astype(jnp.bfloat16)
    g_ref[...] = jnp.dot(h, wg_ref[...], preferred_element_type=jnp.float32)
    for t in range(PROJ_ROWS // ATT):
        ht = h[t * ATT:(t + 1) * ATT]
        qt = lax.dot_general(wqt_ref[...], ht, _NT, preferred_element_type=jnp.float32)
        qt_ref[t] = (qt * q_scale).astype(jnp.bfloat16)
        vt = lax.dot_general(wvt_ref[...], ht, _NT, preferred_element_type=jnp.float32)
        vt_ref[t] = vt.astype(jnp.bfloat16)


def _in_proj(x2, gain, w, q_scale):
    m, d = x2.shape
    wb = w.astype(jnp.bfloat16)
    wqt, wk, wvt, wg = wb[:, 0:d].T, wb[:, d:2 * d], wb[:, 2 * d:3 * d].T, wb[:, 3 * d:4 * d]
    row = pl.BlockSpec((PROJ_ROWS, d), lambda i: (i, 0))
    wspec = pl.BlockSpec((d, d), lambda i: (0, 0))
    tspec = pl.BlockSpec((PROJ_ROWS // ATT, d, ATT), lambda i: (i, 0, 0))
    tshape = jax.ShapeDtypeStruct((m // ATT, d, ATT), jnp.bfloat16)
    return pl.pallas_call(
        functools.partial(_in_proj_kernel, q_scale),
        grid=(m // PROJ_ROWS,),
        in_specs=[row, pl.BlockSpec((1, d), lambda i: (0, 0)), wspec, wspec, wspec, wspec],
        out_specs=[tspec, row, tspec, row],
        out_shape=[tshape, jax.ShapeDtypeStruct((m, d), jnp.bfloat16), tshape,
                   jax.ShapeDtypeStruct((m, d), jnp.float32)],
        compiler_params=pltpu.CompilerParams(
            dimension_semantics=("arbitrary",), vmem_limit_bytes=VMEM_LIMIT),
        name="in_proj",
    )(x2, gain, wqt, wk, wvt, wg)


def _out_proj_kernel(og_ref, w_ref, x_ref, gain_ref, o_ref):
    y = jnp.dot(og_ref[...], w_ref[...], preferred_element_type=jnp.float32)
    o_ref[...] = x_ref[...] + y * _rms_scale(y) * gain_ref[...]


def _out_proj(og2, w, x2, gain):
    m, d = x2.shape
    row = pl.BlockSpec((PROJ_ROWS, d), lambda i: (i, 0))
    return pl.pallas_call(
        _out_proj_kernel,
        grid=(m // PROJ_ROWS,),
        in_specs=[row,
                  pl.BlockSpec((d, d), lambda i: (0, 0)),
                  row,
                  pl.BlockSpec((1, d), lambda i: (0, 0))],
        out_specs=row,
        out_shape=jax.ShapeDtypeStruct((m, d), jnp.float32),
        compiler_params=pltpu.CompilerParams(
            dimension_semantics=("arbitrary",), vmem_limit_bytes=VMEM_LIMIT),
        name="out_proj",
    )(og2, w, x2, gain)


def _split_heads_t(qt):
    row = lax.broadcasted_iota(jnp.int32, (HEAD_PAIR, 1), 0)
    zero = jnp.zeros_like(qt)
    return jnp.where(row < HEAD_DIM, qt, zero), jnp.where(row >= HEAD_DIM, qt, zero)


def _gated(ot, g):
    return (ot.T * (g / (1.0 + jnp.exp(-g)))).astype(jnp.bfloat16)


def _band_bias_kernel(g_ref, o_ref):
    kc = lax.broadcasted_iota(jnp.int32, (ATT, ATT), 0) // CHUNK
    qc = lax.broadcasted_iota(jnp.int32, (ATT, ATT), 1) // CHUNK
    for dlt in range(BAND_TILES):
        g = jnp.broadcast_to(g_ref[0, dlt], (ATT, 2 * ATT))
        toeplitz = pltpu.roll(g, 0, 1, stride=1, stride_axis=0)[:, :ATT]
        back = qc + dlt * (ATT // CHUNK) - kc
        valid = (back >= 0) & (back <= PAST_CHUNKS)
        o_ref[0, dlt] = jnp.where(valid, toeplitz, MASK_VALUE)
    o_ref[0, BAND_TILES] = jnp.full((ATT, ATT), MASK_VALUE, jnp.float32)


def _band_bias_tiles(table):
    nh = table.shape[0]
    m = jnp.arange(2 * ATT)
    rel = jnp.where(m < ATT, m, m - 2 * ATT)[None, :] + ATT * jnp.arange(BAND_TILES)[:, None]
    g = table[:, jnp.clip(rel, -REL_CLIP, REL_CLIP) + REL_CLIP]
    return pl.pallas_call(
        _band_bias_kernel,
        grid=(nh,),
        in_specs=[pl.BlockSpec((1, BAND_TILES, 1, 2 * ATT), lambda h: (h, 0, 0, 0))],
        out_specs=pl.BlockSpec((1, BAND_TILES + 1, ATT, ATT), lambda h: (h, 0, 0, 0)),
        out_shape=jax.ShapeDtypeStruct((nh, BAND_TILES + 1, ATT, ATT), jnp.float32),
        compiler_params=pltpu.CompilerParams(dimension_semantics=("arbitrary",)),
        name="band_bias_tiles",
    )(g[:, :, None, :].astype(jnp.float32))


def _band_kernel(qt_ref, k_ref, vt_ref, g_ref, bm_ref, o_ref, s_ref):
    nq = qt_ref.shape[1]

    def window_start(qi):
        return jnp.maximum(qi - (BAND_TILES - 1), 0)

    def scores(qi):
        t0 = window_start(qi)
        kt = k_ref[0, pl.ds(pl.multiple_of(t0 * ATT, ATT), BAND_TILES * ATT), :]
        return [jnp.dot(kt, qth, preferred_element_type=jnp.float32)
                for qth in _split_heads_t(qt_ref[0, qi])]

    def finish(qi):
        t0 = window_start(qi)
        outs = []
        for h in range(2):
            tiles = []
            for t in range(BAND_TILES):
                dlt = qi - (t0 + t)
                tiles.append(s_ref[h, t * ATT:(t + 1) * ATT]
                             + bm_ref[h, jnp.where(dlt < 0, BAND_TILES, dlt)])
            top = functools.reduce(jnp.maximum,
                                   [jnp.max(x, axis=0, keepdims=True) for x in tiles])
            ps = [jnp.exp(x - top) for x in tiles]
            denom = sum(jnp.sum(p, axis=0, keepdims=True) for p in ps)
            rows = slice(h * HEAD_DIM, (h + 1) * HEAD_DIM)
            o = sum(jnp.dot(vt_ref[0, t0 + t, rows, :], p.astype(jnp.bfloat16),
                            preferred_element_type=jnp.float32) for t, p in enumerate(ps))
            outs.append(o / denom)
        rows = pl.ds(pl.multiple_of(qi * ATT, ATT), ATT)
        o_ref[0, rows, :] = _gated(jnp.concatenate(outs, axis=0), g_ref[0, rows, :])

    def store_scores(ss):
        for h in range(2):
            s_ref[h] = ss[h]

    store_scores(scores(0))

    def body(qi, c):
        ss = scores(qi)
        finish(qi - 1)
        store_scores(ss)
        return c

    lax.fori_loop(1, nq, body, 0)
    finish(nq - 1)


def _band_attention(qt, k, vt, g, bm):
    b, s, d = k.shape
    nq = s // ATT
    tiles = pl.BlockSpec((1, nq, HEAD_PAIR, ATT), lambda hp, bi: (bi, 0, hp, 0))
    seq = pl.BlockSpec((1, s, HEAD_PAIR), lambda hp, bi: (bi, 0, hp))
    return pl.pallas_call(
        _band_kernel,
        grid=(d // HEAD_PAIR, b),
        in_specs=[tiles, seq, tiles, seq,
                  pl.BlockSpec((2, BAND_TILES + 1, ATT, ATT), lambda hp, bi: (hp, 0, 0, 0))],
        out_specs=seq,
        out_shape=jax.ShapeDtypeStruct((b, s, d), jnp.bfloat16),
        scratch_shapes=[pltpu.VMEM((2, BAND_TILES * ATT, ATT), jnp.float32)],
        compiler_params=pltpu.CompilerParams(
            dimension_semantics=("arbitrary",) * 2, vmem_limit_bytes=VMEM_LIMIT),
        name="band_attention",
    )(qt.reshape(b, nq, d, ATT), k, vt.reshape(b, nq, d, ATT), g, bm)


def _softplus2(z):
    neg_abs = pltpu.bitcast(pltpu.bitcast(z, jnp.uint32) | jnp.uint32(0x80000000), jnp.float32)
    return jnp.maximum(z, 0.0) + jnp.log(1.0 + jnp.exp2(neg_abs)) * LOG2E


def _sb_kernel(qt_ref, k_ref, vt_ref, g_ref, o_ref,
               ot_ref, carry_ref, colsum_ref, sp_ref, lb_ref, d_ref):
    qi = pl.program_id(2)
    qts = _split_heads_t(qt_ref[0])
    ss = lax.broadcasted_iota(jnp.int32, (ATT, ATT), 0)
    cc = lax.broadcasted_iota(jnp.int32, (ATT, ATT), 1)
    strict_after = (cc > ss).astype(jnp.bfloat16)
    causal = ss < cc

    def scores_mm(j):
        kt = k_ref[0, pl.ds(pl.multiple_of(j * ATT, ATT), ATT), :]
        return [jnp.dot(kt, qth, preferred_element_type=jnp.float32) for qth in qts]

    def scores_tail(zs, diagonal):
        for h, z in enumerate(zs):
            sp = _softplus2(z)
            lb = z - sp
            if diagonal:
                sp = jnp.where(causal, sp, 0.0)
                lb = jnp.where(causal, lb, MASK_VALUE)
            sp_ref[h] = sp.astype(jnp.bfloat16)
            lb_ref[h] = lb

    def suffix_mm():
        return [jnp.dot(strict_after, sp_ref[h], preferred_element_type=jnp.float32)
                for h in range(2)]

    def suffix_tail(afters):
        for h, after in enumerate(afters):
            d_ref[h] = lb_ref[h] - after
            colsum_ref[h] = after[0:1, :] + sp_ref[h, 0:1, :].astype(jnp.float32)

    def values(j):
        for h in range(2):
            carry = carry_ref[h]
            a = jnp.exp2(d_ref[h]).astype(jnp.bfloat16)
            rows = slice(h * HEAD_DIM, (h + 1) * HEAD_DIM)
            pv = jnp.dot(vt_ref[0, j, rows, :], a, preferred_element_type=jnp.float32)
            ot_ref[rows, :] += pv * jnp.exp2(-carry)
            carry_ref[h] = carry + colsum_ref[h]

    ot_ref[...] = jnp.zeros_like(ot_ref)
    carry_ref[...] = jnp.zeros_like(carry_ref)
    scores_tail(scores_mm(qi), True)

    @pl.when(qi == 0)
    def _single_tile():
        suffix_tail(suffix_mm())
        values(0)

    @pl.when(qi >= 1)
    def _pipelined():
        zs = scores_mm(qi - 1)
        suffix_tail(suffix_mm())
        scores_tail(zs, False)

        def body(t, c):
            zs = scores_mm(qi - 2 - t)
            afters = suffix_mm()
            values(qi - t)
            suffix_tail(afters)
            scores_tail(zs, False)
            return c

        lax.fori_loop(0, qi - 1, body, 0)
        afters = suffix_mm()
        values(1)
        suffix_tail(afters)
        values(0)

    o_ref[0] = _gated(ot_ref[...], g_ref[0])


def _stick_breaking_attention(qt, k, vt, g):
    b, s, d = k.shape
    nq = s // ATT
    return pl.pallas_call(
        _sb_kernel,
        grid=(b, d // HEAD_PAIR, nq),
        in_specs=[pl.BlockSpec((1, HEAD_PAIR, ATT), lambda bi, hp, qi: (bi * nq + qi, hp, 0)),
                  pl.BlockSpec((1, s, HEAD_PAIR), lambda bi, hp, qi: (bi, 0, hp)),
                  pl.BlockSpec((1, nq, HEAD_PAIR, ATT), lambda bi, hp, qi: (bi, 0, hp, 0)),
                  pl.BlockSpec((1, ATT, HEAD_PAIR), lambda bi, hp, qi: (bi, qi, hp))],
        out_specs=pl.BlockSpec((1, ATT, HEAD_PAIR), lambda bi, hp, qi: (bi, qi, hp)),
        out_shape=jax.ShapeDtypeStruct((b, s, d), jnp.bfloat16),
        scratch_shapes=[pltpu.VMEM((HEAD_PAIR, ATT), jnp.float32),
                        pltpu.VMEM((2, 1, ATT), jnp.float32),
                        pltpu.VMEM((2, 1, ATT), jnp.float32),
                        pltpu.VMEM((2, ATT, ATT), jnp.bfloat16),
                        pltpu.VMEM((2, ATT, ATT), jnp.float32),
                        pltpu.VMEM((2, ATT, ATT), jnp.float32)],
        compiler_params=pltpu.CompilerParams(
            dimension_semantics=("arbitrary",) * 3, vmem_limit_bytes=VMEM_LIMIT),
        name="stick_breaking_attention",
    )(qt, k, vt.reshape(b, nq, d, ATT), g)


def kernel(x, norm_pre, w_in, rel_bias, w_out, norm_post):
    b, s, d = x.shape
    depth = w_in.shape[0]
    x2 = x.reshape(b * s, d)
    scale = 1.0 / math.sqrt(HEAD_DIM)
    for i in range(depth):
        band = i % 2 == 0
        qt, k, vt, g = _in_proj(x2, norm_pre[i][None, :], w_in[i],
                                scale if band else scale * LOG2E)
        k, g = k.reshape(b, s, d), g.reshape(b, s, d)
        if band:
            og = _band_attention(qt, k, vt, g, _band_bias_tiles(rel_bias[i // 2]))
        else:
            og = _stick_breaking_attention(qt, k, vt, g)
        x2 = _out_proj(og.reshape(b * s, d), w_out[i].astype(jnp.bfloat16), x2,
                       norm_post[i][None, :])
    return x2.reshape(b, s, d)
```

```python
import functools
import math

import jax
import jax.numpy as jnp
from jax import lax
from jax.experimental import pallas as pl
from jax.experimental.pallas import tpu as pltpu

N_HEADS = 16
HEAD_DIM = 64
CHUNK = 64
PAST_CHUNKS = 8
PAST = PAST_CHUNKS * CHUNK
REL_CLIP = 128
RMS_EPS = 1e-6
MASK_VALUE = -1e30
LOG2E = math.log2(math.e)

HEAD_PAIR = 2 * HEAD_DIM
PROJ_ROWS = 512
ATT = 256
BAND_TILES = PAST // ATT + 1
SB_BLOCKS = 4
VMEM_LIMIT = 48 * 1024 * 1024

_NT = (((1,), (1,)), ((), ()))


def _rms_scale(x):
    return lax.rsqrt(jnp.mean(x * x, axis=-1, keepdims=True) + RMS_EPS)


def _in_proj_kernel(q_scale, x_ref, gain_ref, wqt_ref, wk_ref, wvt_ref, wg_ref,
                    qt_ref, k_ref, vt_ref, g_ref):
    x = x_ref[...]
    h = (x * _rms_scale(x) * gain_ref[...]).astype(jnp.bfloat16)
    k_ref[...] = jnp.dot(h, wk_ref[...], preferred_element_type=jnp.float32).astype(jnp.bfloat16)
    g_ref[...] = jnp.dot(h, wg_ref[...], preferred_element_type=jnp.float32)
    for t in range(PROJ_ROWS // ATT):
        ht = h[t * ATT:(t + 1) * ATT]
        qt = lax.dot_general(wqt_ref[...], ht, _NT, preferred_element_type=jnp.float32)
        qt_ref[t] = (qt * q_scale).astype(jnp.bfloat16)
        vt = lax.dot_general(wvt_ref[...], ht, _NT, preferred_element_type=jnp.float32)
        vt_ref[t] = vt.astype(jnp.bfloat16)


def _in_proj(x2, gain, w, q_scale):
    m, d = x2.shape
    wb = w.astype(jnp.bfloat16)
    wqt, wk, wvt, wg = wb[:, 0:d].T, wb[:, d:2 * d], wb[:, 2 * d:3 * d].T, wb[:, 3 * d:4 * d]
    row = pl.BlockSpec((PROJ_ROWS, d), lambda i: (i, 0))
    wspec = pl.BlockSpec((d, d), lambda i: (0, 0))
    tspec = pl.BlockSpec((PROJ_ROWS // ATT, d, ATT), lambda i: (i, 0, 0))
    tshape = jax.ShapeDtypeStruct((m // ATT, d, ATT), jnp.bfloat16)
    return pl.pallas_call(
        functools.partial(_in_proj_kernel, q_scale),
        grid=(m // PROJ_ROWS,),
        in_specs=[row, pl.BlockSpec((1, d), lambda i: (0, 0)), wspec, wspec, wspec, wspec],
        out_specs=[tspec, row, tspec, row],
        out_shape=[tshape, jax.ShapeDtypeStruct((m, d), jnp.bfloat16), tshape,
                   jax.ShapeDtypeStruct((m, d), jnp.float32)],
        compiler_params=pltpu.CompilerParams(
            dimension_semantics=("arbitrary",), vmem_limit_bytes=VMEM_LIMIT),
        name="in_proj",
    )(x2, gain, wqt, wk, wvt, wg)


def _out_proj_kernel(og_ref, w_ref, x_ref, gain_ref, o_ref):
    y = jnp.dot(og_ref[...], w_ref[...], preferred_element_type=jnp.float32)
    o_ref[...] = x_ref[...] + y * _rms_scale(y) * gain_ref[...]


def _out_proj(og2, w, x2, gain):
    m, d = x2.shape
    row = pl.BlockSpec((PROJ_ROWS, d), lambda i: (i, 0))
    return pl.pallas_call(
        _out_proj_kernel,
        grid=(m // PROJ_ROWS,),
        in_specs=[row,
                  pl.BlockSpec((d, d), lambda i: (0, 0)),
                  row,
                  pl.BlockSpec((1, d), lambda i: (0, 0))],
        out_specs=row,
        out_shape=jax.ShapeDtypeStruct((m, d), jnp.float32),
        compiler_params=pltpu.CompilerParams(
            dimension_semantics=("arbitrary",), vmem_limit_bytes=VMEM_LIMIT),
        name="out_proj",
    )(og2, w, x2, gain)


def _split_heads_t(qt):
    row = lax.broadcasted_iota(jnp.int32, (HEAD_PAIR, 1), 0)
    zero = jnp.zeros_like(qt)
    return jnp.where(row < HEAD_DIM, qt, zero), jnp.where(row >= HEAD_DIM, qt, zero)


def _gated(ot, g):
    return (ot.T * (g / (1.0 + jnp.exp(-g)))).astype(jnp.bfloat16)


def _band_bias_kernel(g_ref, o_ref):
    kc = lax.broadcasted_iota(jnp.int32, (ATT, ATT), 0) // CHUNK
    qc = lax.broadcasted_iota(jnp.int32, (ATT, ATT), 1) // CHUNK
    for dlt in range(BAND_TILES):
        g = jnp.broadcast_to(g_ref[0, dlt], (ATT, 2 * ATT))
        toeplitz = pltpu.roll(g, 0, 1, stride=1, stride_axis=0)[:, :ATT]
        back = qc + dlt * (ATT // CHUNK) - kc
        valid = (back >= 0) & (back <= PAST_CHUNKS)
        o_ref[0, dlt] = jnp.where(valid, toeplitz, MASK_VALUE)
    o_ref[0, BAND_TILES] = jnp.full((ATT, ATT), MASK_VALUE, jnp.float32)


def _band_bias_tiles(table):
    nh = table.shape[0]
    m = jnp.arange(2 * ATT)
    rel = jnp.where(m < ATT, m, m - 2 * ATT)[None, :] + ATT * jnp.arange(BAND_TILES)[:, None]
    g = table[:, jnp.clip(rel, -REL_CLIP, REL_CLIP) + REL_CLIP]
    return pl.pallas_call(
        _band_bias_kernel,
        grid=(nh,),
        in_specs=[pl.BlockSpec((1, BAND_TILES, 1, 2 * ATT), lambda h: (h, 0, 0, 0))],
        out_specs=pl.BlockSpec((1, BAND_TILES + 1, ATT, ATT), lambda h: (h, 0, 0, 0)),
        out_shape=jax.ShapeDtypeStruct((nh, BAND_TILES + 1, ATT, ATT), jnp.float32),
        compiler_params=pltpu.CompilerParams(dimension_semantics=("arbitrary",)),
        name="band_bias_tiles",
    )(g[:, :, None, :].astype(jnp.float32))


def _band_kernel(qt_ref, k_ref, vt_ref, g_ref, bm_ref, o_ref, s_ref):
    nq = qt_ref.shape[1]

    def window_start(qi):
        return jnp.maximum(qi - (BAND_TILES - 1), 0)

    def scores(qi):
        t0 = window_start(qi)
        kt = k_ref[0, pl.ds(pl.multiple_of(t0 * ATT, ATT), BAND_TILES * ATT), :]
        return [jnp.dot(kt, qth, preferred_element_type=jnp.float32)
                for qth in _split_heads_t(qt_ref[0, qi])]

    def finish(qi):
        t0 = window_start(qi)
        outs = []
        for h in range(2):
            tiles = []
            for t in range(BAND_TILES):
                dlt = qi - (t0 + t)
                tiles.append(s_ref[h, t * ATT:(t + 1) * ATT]
                             + bm_ref[h, jnp.where(dlt < 0, BAND_TILES, dlt)])
            top = functools.reduce(jnp.maximum,
                                   [jnp.max(x, axis=0, keepdims=True) for x in tiles])
            ps = [jnp.exp(x - top) for x in tiles]
            denom = sum(jnp.sum(p, axis=0, keepdims=True) for p in ps)
            rows = slice(h * HEAD_DIM, (h + 1) * HEAD_DIM)
            o = sum(jnp.dot(vt_ref[0, t0 + t, rows, :], p.astype(jnp.bfloat16),
                            preferred_element_type=jnp.float32) for t, p in enumerate(ps))
            outs.append(o / denom)
        rows = pl.ds(pl.multiple_of(qi * ATT, ATT), ATT)
        o_ref[0, rows, :] = _gated(jnp.concatenate(outs, axis=0), g_ref[0, rows, :])

    def store_scores(ss):
        for h in range(2):
            s_ref[h] = ss[h]

    store_scores(scores(0))

    def body(qi, c):
        ss = scores(qi)
        finish(qi - 1)
        store_scores(ss)
        return c

    lax.fori_loop(1, nq, body, 0)
    finish(nq - 1)


def _band_attention(qt, k, vt, g, bm):
    b, s, d = k.shape
    nq = s // ATT
    tiles = pl.BlockSpec((1, nq, HEAD_PAIR, ATT), lambda hp, bi: (bi, 0, hp, 0))
    seq = pl.BlockSpec((1, s, HEAD_PAIR), lambda hp, bi: (bi, 0, hp))
    return pl.pallas_call(
        _band_kernel,
        grid=(d // HEAD_PAIR, b),
        in_specs=[tiles, seq, tiles, seq,
                  pl.BlockSpec((2, BAND_TILES + 1, ATT, ATT), lambda hp, bi: (hp, 0, 0, 0))],
        out_specs=seq,
        out_shape=jax.ShapeDtypeStruct((b, s, d), jnp.bfloat16),
        scratch_shapes=[pltpu.VMEM((2, BAND_TILES * ATT, ATT), jnp.float32)],
        compiler_params=pltpu.CompilerParams(
            dimension_semantics=("arbitrary",) * 2, vmem_limit_bytes=VMEM_LIMIT),
        name="band_attention",
    )(qt.reshape(b, nq, d, ATT), k, vt.reshape(b, nq, d, ATT), g, bm)


def _softplus2(z):
    neg_abs = pltpu.bitcast(pltpu.bitcast(z, jnp.uint32) | jnp.uint32(0x80000000), jnp.float32)
    return jnp.maximum(z, 0.0) + jnp.log(1.0 + jnp.exp2(neg_abs)) * LOG2E


def _sb_kernel(qt_ref, k_ref, vt_ref, g_ref, o_ref,
               ot_ref, carry_ref, colsum_ref, sp_ref, lb_ref, d_ref):
    ss = lax.broadcasted_iota(jnp.int32, (ATT, ATT), 0)
    cc = lax.broadcasted_iota(jnp.int32, (ATT, ATT), 1)
    strict_after = (cc > ss).astype(jnp.bfloat16)
    causal = ss < cc

    def suffix_mm():
        return [jnp.dot(strict_after, sp_ref[h], preferred_element_type=jnp.float32)
                for h in range(2)]

    def suffix_tail(afters):
        for h, after in enumerate(afters):
            d_ref[h] = lb_ref[h] - after
            colsum_ref[h] = after[0:1, :] + sp_ref[h, 0:1, :].astype(jnp.float32)

    def values(j):
        for h in range(2):
            carry = carry_ref[h]
            a = jnp.exp2(d_ref[h]).astype(jnp.bfloat16)
            rows = slice(h * HEAD_DIM, (h + 1) * HEAD_DIM)
            pv = jnp.dot(vt_ref[0, j, rows, :], a, preferred_element_type=jnp.float32)
            ot_ref[rows, :] += pv * jnp.exp2(-carry)
            carry_ref[h] = carry + colsum_ref[h]

    def query_block(qi, qt, out_rows):
        qts = _split_heads_t(qt)

        def scores_mm(j):
            kt = k_ref[0, pl.ds(pl.multiple_of(j * ATT, ATT), ATT), :]
            return [jnp.dot(kt, qth, preferred_element_type=jnp.float32) for qth in qts]

        def scores_tail(zs, diagonal):
            for h, z in enumerate(zs):
                sp = _softplus2(z)
                lb = z - sp
                if diagonal:
                    sp = jnp.where(causal, sp, 0.0)
                    lb = jnp.where(causal, lb, MASK_VALUE)
                sp_ref[h] = sp.astype(jnp.bfloat16)
                lb_ref[h] = lb

        ot_ref[...] = jnp.zeros_like(ot_ref)
        carry_ref[...] = jnp.zeros_like(carry_ref)

        scores_tail(scores_mm(qi), True)

        @pl.when(qi == 0)
        def _single_tile():
            suffix_tail(suffix_mm())
            values(0)

        @pl.when(qi >= 1)
        def _pipelined():
            zs = scores_mm(qi - 1)
            suffix_tail(suffix_mm())
            scores_tail(zs, False)

            def body(t, c):
                zs = scores_mm(qi - 2 - t)
                afters = suffix_mm()
                values(qi - t)
                suffix_tail(afters)
                scores_tail(zs, False)
                return c

            lax.fori_loop(0, qi - 1, body, 0)
            afters = suffix_mm()
            values(1)
            suffix_tail(afters)
            values(0)

        o_ref[0, out_rows, :] = _gated(ot_ref[...], g_ref[0, out_rows, :])

    for sub in range(SB_BLOCKS):
        query_block(pl.program_id(2) * SB_BLOCKS + sub, qt_ref[sub],
                    slice(sub * ATT, (sub + 1) * ATT))


def _stick_breaking_attention(qt, k, vt, g):
    b, s, d = k.shape
    nq = s // ATT
    blk = pl.BlockSpec((1, SB_BLOCKS * ATT, HEAD_PAIR), lambda bi, hp, qi: (bi, qi, hp))
    return pl.pallas_call(
        _sb_kernel,
        grid=(b, d // HEAD_PAIR, nq // SB_BLOCKS),
        in_specs=[pl.BlockSpec((SB_BLOCKS, HEAD_PAIR, ATT),
                               lambda bi, hp, qi: (bi * (nq // SB_BLOCKS) + qi, hp, 0)),
                  pl.BlockSpec((1, s, HEAD_PAIR), lambda bi, hp, qi: (bi, 0, hp)),
                  pl.BlockSpec((1, nq, HEAD_PAIR, ATT), lambda bi, hp, qi: (bi, 0, hp, 0)),
                  blk],
        out_specs=blk,
        out_shape=jax.ShapeDtypeStruct((b, s, d), jnp.bfloat16),
        scratch_shapes=[pltpu.VMEM((HEAD_PAIR, ATT), jnp.float32),
                        pltpu.VMEM((2, 1, ATT), jnp.float32),
                        pltpu.VMEM((2, 1, ATT), jnp.float32),
                        pltpu.VMEM((2, ATT, ATT), jnp.bfloat16),
                        pltpu.VMEM((2, ATT, ATT), jnp.float32),
                        pltpu.VMEM((2, ATT, ATT), jnp.float32)],
        compiler_params=pltpu.CompilerParams(
            dimension_semantics=("arbitrary",) * 3, vmem_limit_bytes=VMEM_LIMIT),
        name="stick_breaking_attention",
    )(qt, k, vt.reshape(b, nq, d, ATT), g)


def kernel(x, norm_pre, w_in, rel_bias, w_out, norm_post):
    b, s, d = x.shape
    depth = w_in.shape[0]
    x2 = x.reshape(b * s, d)
    scale = 1.0 / math.sqrt(HEAD_DIM)
    for i in range(depth):
        band = i % 2 == 0
        qt, k, vt, g = _in_proj(x2, norm_pre[i][None, :], w_in[i],
                                scale if band else scale * LOG2E)
        k, g = k.reshape(b, s, d), g.reshape(b, s, d)
        if band:
            og = _band_attention(qt, k, vt, g, _band_bias_tiles(rel_bias[i // 2]))
        else:
            og = _stick_breaking_attention(qt, k, vt, g)
        x2 = _out_proj(og.reshape(b * s, d), w_out[i].astype(jnp.bfloat16), x2,
                       norm_post[i][None, :])
    return x2.reshape(b, s, d)
```

```python
import functools
import math

import jax
import jax.numpy as jnp
from jax import lax
from jax.experimental import pallas as pl
from jax.experimental.pallas import tpu as pltpu

N_HEADS = 16
HEAD_DIM = 64
CHUNK = 64
PAST_CHUNKS = 8
PAST = PAST_CHUNKS * CHUNK
REL_CLIP = 128
RMS_EPS = 1e-6
MASK_VALUE = -1e30
LOG2E = math.log2(math.e)

HEAD_PAIR = 2 * HEAD_DIM
PROJ_ROWS = 512
ATT = 256
BAND_TILES = PAST // ATT + 1
SB_BLOCKS = 4
VMEM_LIMIT = 48 * 1024 * 1024

_NT = (((1,), (1,)), ((), ()))


def _rms_scale(x):
    return lax.rsqrt(jnp.mean(x * x, axis=-1, keepdims=True) + RMS_EPS)


def _in_proj_kernel(q_scale, x_ref, gain_ref, wqt_ref, wk_ref, wvt_ref, wg_ref,
                    qt_ref, k_ref, vt_ref, g_ref):
    x = x_ref[...]
    h = (x * _rms_scale(x) * gain_ref[...]).astype(jnp.bfloat16)
    k_ref[...] = jnp.dot(h, wk_ref[...], preferred_element_type=jnp.float32).astype(jnp.bfloat16)
    g_ref[...] = jnp.dot(h, wg_ref[...], preferred_element_type=jnp.float32)
    for t in range(PROJ_ROWS // ATT):
        ht = h[t * ATT:(t + 1) * ATT]
        qt = lax.dot_general(wqt_ref[...], ht, _NT, preferred_element_type=jnp.float32)
        qt_ref[t] = (qt * q_scale).astype(jnp.bfloat16)
        vt = lax.dot_general(wvt_ref[...], ht, _NT, preferred_element_type=jnp.float32)
        vt_ref[t] = vt.astype(jnp.bfloat16)


def _in_proj(x2, gain, w, q_scale):
    m, d = x2.shape
    wb = w.astype(jnp.bfloat16)
    wqt, wk, wvt, wg = wb[:, 0:d].T, wb[:, d:2 * d], wb[:, 2 * d:3 * d].T, wb[:, 3 * d:4 * d]
    row = pl.BlockSpec((PROJ_ROWS, d), lambda i: (i, 0))
    wspec = pl.BlockSpec((d, d), lambda i: (0, 0))
    tspec = pl.BlockSpec((PROJ_ROWS // ATT, d, ATT), lambda i: (i, 0, 0))
    tshape = jax.ShapeDtypeStruct((m // ATT, d, ATT), jnp.bfloat16)
    return pl.pallas_call(
        functools.partial(_in_proj_kernel, q_scale),
        grid=(m // PROJ_ROWS,),
        in_specs=[row, pl.BlockSpec((1, d), lambda i: (0, 0)), wspec, wspec, wspec, wspec],
        out_specs=[tspec, row, tspec, row],
        out_shape=[tshape, jax.ShapeDtypeStruct((m, d), jnp.bfloat16), tshape,
                   jax.ShapeDtypeStruct((m, d), jnp.float32)],
        compiler_params=pltpu.CompilerParams(
            dimension_semantics=("arbitrary",), vmem_limit_bytes=VMEM_LIMIT),
        name="in_proj",
    )(x2, gain, wqt, wk, wvt, wg)


def _out_proj_kernel(og_ref, w_ref, x_ref, gain_ref, o_ref):
    y = jnp.dot(og_ref[...], w_ref[...], preferred_element_type=jnp.float32)
    o_ref[...] = x_ref[...] + y * _rms_scale(y) * gain_ref[...]


def _out_proj(og2, w, x2, gain):
    m, d = x2.shape
    row = pl.BlockSpec((PROJ_ROWS, d), lambda i: (i, 0))
    return pl.pallas_call(
        _out_proj_kernel,
        grid=(m // PROJ_ROWS,),
        in_specs=[row,
                  pl.BlockSpec((d, d), lambda i: (0, 0)),
                  row,
                  pl.BlockSpec((1, d), lambda i: (0, 0))],
        out_specs=row,
        out_shape=jax.ShapeDtypeStruct((m, d), jnp.float32),
        compiler_params=pltpu.CompilerParams(
            dimension_semantics=("arbitrary",), vmem_limit_bytes=VMEM_LIMIT),
        name="out_proj",
    )(og2, w, x2, gain)


def _split_heads_t(qt):
    row = lax.broadcasted_iota(jnp.int32, (HEAD_PAIR, 1), 0)
    zero = jnp.zeros_like(qt)
    return jnp.where(row < HEAD_DIM, qt, zero), jnp.where(row >= HEAD_DIM, qt, zero)


def _gated(ot, g):
    return (ot.T * (g / (1.0 + jnp.exp(-g)))).astype(jnp.bfloat16)


def _band_bias_kernel(g_ref, o_ref):
    kc = lax.broadcasted_iota(jnp.int32, (ATT, ATT), 0) // CHUNK
    qc = lax.broadcasted_iota(jnp.int32, (ATT, ATT), 1) // CHUNK
    for dlt in range(BAND_TILES):
        g = jnp.broadcast_to(g_ref[0, dlt], (ATT, 2 * ATT))
        toeplitz = pltpu.roll(g, 0, 1, stride=1, stride_axis=0)[:, :ATT]
        back = qc + dlt * (ATT // CHUNK) - kc
        valid = (back >= 0) & (back <= PAST_CHUNKS)
        o_ref[0, dlt] = jnp.where(valid, toeplitz, MASK_VALUE)
    o_ref[0, BAND_TILES] = jnp.full((ATT, ATT), MASK_VALUE, jnp.float32)


def _band_bias_tiles(table):
    nh = table.shape[0]
    m = jnp.arange(2 * ATT)
    rel = jnp.where(m < ATT, m, m - 2 * ATT)[None, :] + ATT * jnp.arange(BAND_TILES)[:, None]
    g = table[:, jnp.clip(rel, -REL_CLIP, REL_CLIP) + REL_CLIP]
    return pl.pallas_call(
        _band_bias_kernel,
        grid=(nh,),
        in_specs=[pl.BlockSpec((1, BAND_TILES, 1, 2 * ATT), lambda h: (h, 0, 0, 0))],
        out_specs=pl.BlockSpec((1, BAND_TILES + 1, ATT, ATT), lambda h: (h, 0, 0, 0)),
        out_shape=jax.ShapeDtypeStruct((nh, BAND_TILES + 1, ATT, ATT), jnp.float32),
        compiler_params=pltpu.CompilerParams(dimension_semantics=("arbitrary",)),
        name="band_bias_tiles",
    )(g[:, :, None, :].astype(jnp.float32))


def _band_kernel(qt_ref, k_ref, vt_ref, g_ref, bm_ref, o_ref, s_ref):
    nq = qt_ref.shape[1]

    def window_start(qi):
        return jnp.maximum(qi - (BAND_TILES - 1), 0)

    def scores(qi):
        t0 = window_start(qi)
        kt = k_ref[0, pl.ds(pl.multiple_of(t0 * ATT, ATT), BAND_TILES * ATT), :]
        return [jnp.dot(kt, qth, preferred_element_type=jnp.float32)
                for qth in _split_heads_t(qt_ref[0, qi])]

    def finish(qi):
        t0 = window_start(qi)
        outs = []
        for h in range(2):
            tiles = []
            for t in range(BAND_TILES):
                dlt = qi - (t0 + t)
                tiles.append(s_ref[h, t * ATT:(t + 1) * ATT]
                             + bm_ref[h, jnp.where(dlt < 0, BAND_TILES, dlt)])
            top = functools.reduce(jnp.maximum,
                                   [jnp.max(x, axis=0, keepdims=True) for x in tiles])
            ps = [jnp.exp(x - top) for x in tiles]
            denom = sum(jnp.sum(p, axis=0, keepdims=True) for p in ps)
            rows = slice(h * HEAD_DIM, (h + 1) * HEAD_DIM)
            o = sum(jnp.dot(vt_ref[0, t0 + t, rows, :], p.astype(jnp.bfloat16),
                            preferred_element_type=jnp.float32) for t, p in enumerate(ps))
            outs.append(o / denom)
        rows = pl.ds(pl.multiple_of(qi * ATT, ATT), ATT)
        o_ref[0, rows, :] = _gated(jnp.concatenate(outs, axis=0), g_ref[0, rows, :])

    def store_scores(ss):
        for h in range(2):
            s_ref[h] = ss[h]

    store_scores(scores(0))

    def body(qi, c):
        ss = scores(qi)
        finish(qi - 1)
        store_scores(ss)
        return c

    lax.fori_loop(1, nq, body, 0)
    finish(nq - 1)


def _band_attention(qt, k, vt, g, bm):
    b, s, d = k.shape
    nq = s // ATT
    tiles = pl.BlockSpec((1, nq, HEAD_PAIR, ATT), lambda hp, bi: (bi, 0, hp, 0))
    seq = pl.BlockSpec((1, s, HEAD_PAIR), lambda hp, bi: (bi, 0, hp))
    return pl.pallas_call(
        _band_kernel,
        grid=(d // HEAD_PAIR, b),
        in_specs=[tiles, seq, tiles, seq,
                  pl.BlockSpec((2, BAND_TILES + 1, ATT, ATT), lambda hp, bi: (hp, 0, 0, 0))],
        out_specs=seq,
        out_shape=jax.ShapeDtypeStruct((b, s, d), jnp.bfloat16),
        scratch_shapes=[pltpu.VMEM((2, BAND_TILES * ATT, ATT), jnp.float32)],
        compiler_params=pltpu.CompilerParams(
            dimension_semantics=("arbitrary",) * 2, vmem_limit_bytes=VMEM_LIMIT),
        name="band_attention",
    )(qt.reshape(b, nq, d, ATT), k, vt.reshape(b, nq, d, ATT), g, bm)


def _softplus2(z):
    neg_abs = pltpu.bitcast(pltpu.bitcast(z, jnp.uint32) | jnp.uint32(0x80000000), jnp.float32)
    return jnp.maximum(z, 0.0) + jnp.log(1.0 + jnp.exp2(neg_abs)) * LOG2E


def _sb_kernel(qt_ref, k_ref, vt_ref, g_ref, o_ref,
               ot_ref, carry_ref, colsum_ref, sp_ref, lb_ref, d_ref):
    ss = lax.broadcasted_iota(jnp.int32, (ATT, ATT), 0)
    cc = lax.broadcasted_iota(jnp.int32, (ATT, ATT), 1)
    strict_after = (cc > ss).astype(jnp.bfloat16)
    causal = ss < cc

    def suffix_mm():
        return [jnp.dot(strict_after, sp_ref[h], preferred_element_type=jnp.float32)
                for h in range(2)]

    def suffix_tail(afters):
        for h, after in enumerate(afters):
            d_ref[h] = lb_ref[h] - after
            colsum_ref[h] = after[0:1, :] + sp_ref[h, 0:1, :].astype(jnp.float32)

    def values(j):
        for h in range(2):
            carry = carry_ref[h]
            a = jnp.exp2(d_ref[h]).astype(jnp.bfloat16)
            rows = slice(h * HEAD_DIM, (h + 1) * HEAD_DIM)
            pv = jnp.dot(vt_ref[0, j, rows, :], a, preferred_element_type=jnp.float32)
            ot_ref[rows, :] += pv * jnp.exp2(-carry)
            carry_ref[h] = carry + colsum_ref[h]

    def query_block(qi, qt, out_rows):
        qts = _split_heads_t(qt)

        def scores_mm(j):
            kt = k_ref[0, pl.ds(pl.multiple_of(j * ATT, ATT), ATT), :]
            return [jnp.dot(kt, qth, preferred_element_type=jnp.float32) for qth in qts]

        def scores_tail(zs, diagonal):
            for h, z in enumerate(zs):
                sp = _softplus2(z)
                lb = z - sp
                if diagonal:
                    sp = jnp.where(causal, sp, 0.0)
                    lb = jnp.where(causal, lb, MASK_VALUE)
                sp_ref[h] = sp.astype(jnp.bfloat16)
                lb_ref[h] = lb

        ot_ref[...] = jnp.zeros_like(ot_ref)
        carry_ref[...] = jnp.zeros_like(carry_ref)

        scores_tail(scores_mm(qi), True)

        @pl.when(qi == 0)
        def _single_tile():
            suffix_tail(suffix_mm())
            values(0)

        @pl.when(qi >= 1)
        def _pipelined():
            zs = scores_mm(qi - 1)
            suffix_tail(suffix_mm())
            scores_tail(zs, False)

            def body(t, c):
                kt = k_ref[0, pl.ds(pl.multiple_of((qi - 2 - t) * ATT, ATT), ATT), :]
                zs, afters = [], []
                for h in range(2):
                    zs.append(jnp.dot(kt, qts[h], preferred_element_type=jnp.float32))
                    afters.append(jnp.dot(strict_after, sp_ref[h], preferred_element_type=jnp.float32))
                values(qi - t)
                suffix_tail(afters)
                scores_tail(zs, False)
                return c

            lax.fori_loop(0, qi - 1, body, 0)
            afters = suffix_mm()
            values(1)
            suffix_tail(afters)
            values(0)

        o_ref[0, out_rows, :] = _gated(ot_ref[...], g_ref[0, out_rows, :])

    for sub in range(SB_BLOCKS):
        query_block(pl.program_id(2) * SB_BLOCKS + sub, qt_ref[sub],
                    slice(sub * ATT, (sub + 1) * ATT))


def _stick_breaking_attention(qt, k, vt, g):
    b, s, d = k.shape
    nq = s // ATT
    blk = pl.BlockSpec((1, SB_BLOCKS * ATT, HEAD_PAIR), lambda bi, hp, qi: (bi, qi, hp))
    return pl.pallas_call(
        _sb_kernel,
        grid=(b, d // HEAD_PAIR, nq // SB_BLOCKS),
        in_specs=[pl.BlockSpec((SB_BLOCKS, HEAD_PAIR, ATT),
                               lambda bi, hp, qi: (bi * (nq // SB_BLOCKS) + qi, hp, 0)),
                  pl.BlockSpec((1, s, HEAD_PAIR), lambda bi, hp, qi: (bi, 0, hp)),
                  pl.BlockSpec((1, nq, HEAD_PAIR, ATT), lambda bi, hp, qi: (bi, 0, hp, 0)),
                  blk],
        out_specs=blk,
        out_shape=jax.ShapeDtypeStruct((b, s, d), jnp.bfloat16),
        scratch_shapes=[pltpu.VMEM((HEAD_PAIR, ATT), jnp.float32),
                        pltpu.VMEM((2, 1, ATT), jnp.float32),
                        pltpu.VMEM((2, 1, ATT), jnp.float32),
                        pltpu.VMEM((2, ATT, ATT), jnp.bfloat16),
                        pltpu.VMEM((2, ATT, ATT), jnp.float32),
                        pltpu.VMEM((2, ATT, ATT), jnp.float32)],
        compiler_params=pltpu.CompilerParams(
            dimension_semantics=("arbitrary",) * 3, vmem_limit_bytes=VMEM_LIMIT),
        name="stick_breaking_attention",
    )(qt, k, vt.reshape(b, nq, d, ATT), g)


def kernel(x, norm_pre, w_in, rel_bias, w_out, norm_post):
    b, s, d = x.shape
    depth = w_in.shape[0]
    x2 = x.reshape(b * s, d)
    scale = 1.0 / math.sqrt(HEAD_DIM)
    for i in range(depth):
        band = i % 2 == 0
        qt, k, vt, g = _in_proj(x2, norm_pre[i][None, :], w_in[i],
                                scale if band else scale * LOG2E)
        k, g = k.reshape(b, s, d), g.reshape(b, s, d)
        if band:
            og = _band_attention(qt, k, vt, g, _band_bias_tiles(rel_bias[i // 2]))
        else:
            og = _stick_breaking_attention(qt, k, vt, g)
        x2 = _out_proj(og.reshape(b * s, d), w_out[i].astype(jnp.bfloat16), x2,
                       norm_post[i][None, :])
    return x2.reshape(b, s, d)
```
